```python
import jax, jax.numpy as jnp
from jax import lax
import numpy as np

D_MODEL = 2048
BATCH = 2
SEQ = 16384
DEPTH = 1
DEC_BATCH = 8
DEC_SEQ = 32
PAST_LEN = 2048

CHUNK = 64
D_CONV_A = D_MODEL // 2
D_CONV_B = D_MODEL // 2
CONV_A_WIDTH = 31
CONV_B_WIDTH = 3
N_MEM = 256
N_XHEADS = 4
XHEAD_DIM = D_MODEL // N_XHEADS
D_FF = -(-(8 * D_MODEL) // (3 * 256)) * 256
RMS_EPS = 1e-6
LN_EPS = 1e-5
IN_SIZES = (D_CONV_A, D_CONV_A, D_CONV_B, D_CONV_B, D_CONV_B, D_MODEL, D_MODEL)
IN_SPLITS = tuple(int(v) for v in np.cumsum(IN_SIZES)[:-1])
D_IN = int(sum(IN_SIZES))

kernel_name = "gated_conv_hybrid_stream_step"


def _rms_norm(x, g):
    x32 = x.astype(jnp.float32)
    y = x32 * lax.rsqrt(jnp.mean(jnp.square(x32), axis=-1, keepdims=True) + RMS_EPS)
    return y.astype(x.dtype) * g


def _layer_norm(x, g, b):
    x32 = x.astype(jnp.float32)
    mu = jnp.mean(x32, axis=-1, keepdims=True)
    xc = x32 - mu
    var = jnp.mean(jnp.square(xc), axis=-1, keepdims=True)
    return (xc * lax.rsqrt(var + LN_EPS)).astype(x.dtype) * g + b


def _causal_dwconv(u, ctx, w):
    k = w.shape[0]
    full = jnp.concatenate([ctx.astype(u.dtype), u], axis=1)
    out = lax.conv_general_dilated(full, w[:, None, :].astype(u.dtype), window_strides=(1,),
                                   padding='VALID', dimension_numbers=('NWC', 'WIO', 'NWC'),
                                   feature_group_count=u.shape[-1])
    return out, full[:, -(k - 1):, :]


def _memory_kv(mem, g_mem, w_k, w_v):
    b = mem.shape[0]
    mn = _rms_norm(mem, g_mem)
    k = (mn @ w_k).reshape(b, N_MEM, N_XHEADS, XHEAD_DIM)
    v = (mn @ w_v).reshape(b, N_MEM, N_XHEADS, XHEAD_DIM)
    return k, v


def _layer(x, ctx_a, ctx_b, mem_k, mem_v, lw):
    b, t, _ = x.shape
    u = _rms_norm(x, lw['norm_mix_g'])
    z = u @ lw['w_in'] + lw['b_in']
    a_val, a_gate, b_b, b_c, b_x, g_a, g_b = jnp.split(z, IN_SPLITS, axis=-1)
    a = a_val * jax.nn.sigmoid(a_gate)
    a, new_a = _causal_dwconv(a, ctx_a, lw['conv_a_w'])
    a = _layer_norm(a + lw['conv_a_b'], lw['ln_a_g'], lw['ln_a_b'])
    a = jax.nn.silu(a) @ lw['w_a_out'] + lw['b_a_out']
    cb, new_b = _causal_dwconv(b_c * b_x, ctx_b, lw['conv_b_w'])
    bb = (b_b * cb) @ lw['w_b_out']
    m = jax.nn.sigmoid(g_a) * a + jax.nn.sigmoid(g_b) * bb
    x = x + m @ lw['w_mix_out']
    q = (_rms_norm(x, lw['norm_x_g']) @ lw['w_q']).reshape(b, t, N_XHEADS, XHEAD_DIM)
    s = jnp.einsum('bthd,bmhd->bhtm', q.astype(jnp.float32), mem_k.astype(jnp.float32)) * (XHEAD_DIM ** -0.5)
    p = jax.nn.softmax(s, axis=-1)
    o = jnp.einsum('bhtm,bmhd->bthd', p, mem_v.astype(jnp.float32)).astype(x.dtype).reshape(b, t, D_MODEL)
    x = x + o @ lw['w_o']
    h = _rms_norm(x, lw['norm_ffn_g'])
    x = x + (jax.nn.silu(h @ lw['w_gate']) * (h @ lw['w_up'])) @ lw['w_down']
    return x, new_a, new_b


def setup_inputs(seed: int = 0) -> dict:
    key = jax.random.key(seed)
    ks = jax.random.split(key, 30)
    f32 = jnp.float32

    def nrm(k, shape, scale):
        return jax.random.normal(k, shape, f32) * scale

    def gain(k, shape):
        return 1.0 + 0.01 * jax.random.normal(k, shape, f32)

    L = DEPTH
    return {
        'x_prompt': nrm(ks[0], (BATCH, SEQ, D_MODEL), 1.0),
        'x_sample': nrm(ks[1], (DEC_BATCH, DEC_SEQ, D_MODEL), 1.0),
        'state_conv_a': nrm(ks[2], (L, DEC_BATCH, CONV_A_WIDTH - 1, D_CONV_A), 0.5),
        'state_conv_b': nrm(ks[3], (L, DEC_BATCH, CONV_B_WIDTH - 1, D_CONV_B), 0.5),
        'cache_mem_k': nrm(ks[4], (L, DEC_BATCH, N_MEM, N_XHEADS, XHEAD_DIM), 1.0),
        'cache_mem_v': nrm(ks[5], (L, DEC_BATCH, N_MEM, N_XHEADS, XHEAD_DIM), 1.0),
        'mem_prompt': nrm(ks[6], (BATCH, N_MEM, D_MODEL), 1.0),
        'norm_mix_g': gain(ks[7], (L, D_MODEL)),
        'w_in': nrm(ks[8], (L, D_MODEL, D_IN), D_MODEL ** -0.5),
        'b_in': nrm(ks[9], (L, D_IN), 0.01),
        'conv_a_w': nrm(ks[10], (L, CONV_A_WIDTH, D_CONV_A), CONV_A_WIDTH ** -0.5),
        'conv_a_b': nrm(ks[11], (L, D_CONV_A), 0.01),
        'ln_a_g': gain(ks[12], (L, D_CONV_A)),
        'ln_a_b': nrm(ks[13], (L, D_CONV_A), 0.01),
        'w_a_out': nrm(ks[14], (L, D_CONV_A, D_MODEL), D_CONV_A ** -0.5),
        'b_a_out': nrm(ks[15], (L, D_MODEL), 0.01),
        'conv_b_w': nrm(ks[16], (L, CONV_B_WIDTH, D_CONV_B), CONV_B_WIDTH ** -0.5),
        'w_b_out': nrm(ks[17], (L, D_CONV_B, D_MODEL), D_CONV_B ** -0.5),
        'w_mix_out': nrm(ks[18], (L, D_MODEL, D_MODEL), D_MODEL ** -0.5),
        'norm_x_g': gain(ks[19], (L, D_MODEL)),
        'norm_mem_g': gain(ks[20], (L, D_MODEL)),
        'w_q': nrm(ks[21], (L, D_MODEL, D_MODEL), D_MODEL ** -0.5),
        'w_k': nrm(ks[22], (L, D_MODEL, D_MODEL), D_MODEL ** -0.5),
        'w_v': nrm(ks[23], (L, D_MODEL, D_MODEL), D_MODEL ** -0.5),
        'w_o': nrm(ks[24], (L, D_MODEL, D_MODEL), D_MODEL ** -0.5),
        'norm_ffn_g': gain(ks[25], (L, D_MODEL)),
        'w_gate': nrm(ks[26], (L, D_MODEL, D_FF), D_MODEL ** -0.5),
        'w_up': nrm(ks[27], (L, D_MODEL, D_FF), D_MODEL ** -0.5),
        'w_down': nrm(ks[28], (L, D_FF, D_MODEL), D_FF ** -0.5),
        'norm_final_g': gain(ks[29], (D_MODEL,)),
    }


def reference(x_prompt, x_sample, state_conv_a, state_conv_b, cache_mem_k, cache_mem_v, mem_prompt,
              norm_mix_g, w_in, b_in, conv_a_w, conv_a_b, ln_a_g, ln_a_b, w_a_out, b_a_out,
              conv_b_w, w_b_out, w_mix_out, norm_x_g, norm_mem_g, w_q, w_k, w_v, w_o,
              norm_ffn_g, w_gate, w_up, w_down, norm_final_g):
    hp, hs = x_prompt, x_sample
    conv_a_p, conv_b_p, mem_k_p, mem_v_p, conv_a_s, conv_b_s = [], [], [], [], [], []
    for l in range(DEPTH):
        lw = {
            'norm_mix_g': norm_mix_g[l], 'w_in': w_in[l], 'b_in': b_in[l],
            'conv_a_w': conv_a_w[l], 'conv_a_b': conv_a_b[l], 'ln_a_g': ln_a_g[l], 'ln_a_b': ln_a_b[l],
            'w_a_out': w_a_out[l], 'b_a_out': b_a_out[l], 'conv_b_w': conv_b_w[l], 'w_b_out': w_b_out[l],
            'w_mix_out': w_mix_out[l], 'norm_x_g': norm_x_g[l], 'w_q': w_q[l], 'w_o': w_o[l],
            'norm_ffn_g': norm_ffn_g[l], 'w_gate': w_gate[l], 'w_up': w_up[l], 'w_down': w_down[l],
        }
        kp, vp = _memory_kv(mem_prompt, norm_mem_g[l], w_k[l], w_v[l])
        zero_a = jnp.zeros((hp.shape[0], CONV_A_WIDTH - 1, D_CONV_A), hp.dtype)
        zero_b = jnp.zeros((hp.shape[0], CONV_B_WIDTH - 1, D_CONV_B), hp.dtype)
        hp, na_p, nb_p = _layer(hp, zero_a, zero_b, kp, vp, lw)
        hs, na_s, nb_s = _layer(hs, state_conv_a[l], state_conv_b[l], cache_mem_k[l], cache_mem_v[l], lw)
        conv_a_p.append(na_p); conv_b_p.append(nb_p); mem_k_p.append(kp); mem_v_p.append(vp)
        conv_a_s.append(na_s); conv_b_s.append(nb_s)
    y_prompt = _rms_norm(hp, norm_final_g)
    y_sample = _rms_norm(hs, norm_final_g)
    new_conv_a_prompt = jnp.stack(conv_a_p)
    new_conv_b_prompt = jnp.stack(conv_b_p)
    new_mem_k_prompt = jnp.stack(mem_k_p)
    new_mem_v_prompt = jnp.stack(mem_v_p)
    new_conv_a_sample = jnp.stack(conv_a_s)
    new_conv_b_sample = jnp.stack(conv_b_s)
    return (y_prompt, y_sample, new_conv_a_prompt, new_conv_b_prompt, new_mem_k_prompt, new_mem_v_prompt, new_conv_a_sample, new_conv_b_sample)
```

```python
import functools

import jax
import jax.numpy as jnp
from jax import lax
from jax.experimental import pallas as pl
from jax.experimental.pallas import tpu as pltpu

F32 = jnp.float32
BF16 = jnp.bfloat16

RMS_EPS = 1e-6
LN_EPS = 1e-5
N_XHEADS = 4
CONV_A_WIDTH = 31
CONV_B_WIDTH = 3

SUBLANES_F32 = 8
SUBLANES_BF16 = 16
V7X_VMEM_LIMIT_BYTES = 56 * 1024 * 1024

HALO_A = 32
HALO_B = 16
assert HALO_A >= CONV_A_WIDTH - 1 and HALO_A % SUBLANES_BF16 == 0
assert HALO_B >= CONV_B_WIDTH - 1 and HALO_B % SUBLANES_BF16 == 0

NORM_CHUNK = 32
CONV_ROWS = 64
CONV_LANES = 256
IN_BLOCK = 1024
FF_BLOCK = 512
KV_BLOCK = 512


def _sigmoid(x):
    return 0.5 * jnp.tanh(0.5 * x) + 0.5


def _rms(x, g):
    ms = jnp.mean(x * x, axis=-1, keepdims=True)
    return x * lax.rsqrt(ms + RMS_EPS) * g


def _dot(a, b):
    return jnp.dot(a, b, preferred_element_type=F32)


def _norm_rows(dst_ref, src_ref, g_ref, rows):
    chunk = min(NORM_CHUNK, rows)

    def body(r, c):
        r0 = pl.multiple_of(r * chunk, chunk)
        dst_ref[pl.ds(r0, chunk), :] = _rms(src_ref[0, pl.ds(r0, chunk), :], g_ref[...]).astype(BF16)
        return c

    lax.fori_loop(0, rows // chunk, body, 0)


def _params(n_grid_dims):
    return pltpu.CompilerParams(
        dimension_semantics=("arbitrary",) * n_grid_dims,
        vmem_limit_bytes=V7X_VMEM_LIMIT_BYTES,
    )


def _resident(shape):
    return pl.BlockSpec(shape, lambda *_: (0,) * len(shape), pipeline_mode=pl.Buffered(1))


def _memkv_kernel(m_ref, g_ref, wk_ref, wv_ref, k_ref, v_ref, h_ref, *, rows):
    @pl.when(pl.program_id(1) == 0)
    def _():
        _norm_rows(h_ref, m_ref, g_ref, rows)

    h = h_ref[...]
    k_ref[0] = _dot(h, wk_ref[...])
    v_ref[0] = _dot(h, wv_ref[...])


def _memkv(mem, g, wk, wv):
    b, n, d = mem.shape
    out = jax.ShapeDtypeStruct((b, n, d), F32)
    return pl.pallas_call(
        functools.partial(_memkv_kernel, rows=n),
        grid=(b, d // KV_BLOCK),
        in_specs=[
            pl.BlockSpec((1, n, d), lambda i, j: (i, 0, 0)),
            pl.BlockSpec((1, d), lambda i, j: (0, 0)),
            pl.BlockSpec((d, KV_BLOCK), lambda i, j: (0, j)),
            pl.BlockSpec((d, KV_BLOCK), lambda i, j: (0, j)),
        ],
        out_specs=[
            pl.BlockSpec((1, n, KV_BLOCK), lambda i, j: (i, 0, j)),
            pl.BlockSpec((1, n, KV_BLOCK), lambda i, j: (i, 0, j)),
        ],
        out_shape=[out, out],
        scratch_shapes=[pltpu.VMEM((n, d), BF16)],
        compiler_params=_params(2),
        name="memkv",
    )(mem, g, wk, wv)


P_SGA, P_SGB, P_A, P_CX, P_BB = 0, 2, 4, 5, 6
N_IN_STEPS = 7


def _w1_block(j):
    return jnp.where(j == 0, 0, jnp.where(j == 1, 3, jnp.where(j == 2, 2, j + 2)))


def _w2_block(j):
    return jnp.where(j == 0, 1, 4)


def _p_block(j):
    return jnp.where(j < 3, j + P_A, j - 3)


def _inproj_kernel(x_ref, g_ref, w1_ref, w2_ref, b1_ref, b2_ref,
                   p_ref, taila_ref, tailb_ref, u_ref, *, rows):
    j = pl.program_id(2)

    def z1():
        return _dot(u_ref[...], w1_ref[...]) + b1_ref[...]

    def z2():
        return _dot(u_ref[...], w2_ref[...]) + b2_ref[...]

    @pl.when(j == 0)
    def _():
        _norm_rows(u_ref, x_ref, g_ref, rows)
        a = z1() * _sigmoid(z2())
        p_ref[0] = a.astype(BF16)
        taila_ref[0] = a[rows - HALO_A:, :]

    @pl.when(j == 1)
    def _():
        cx = z1() * z2()
        p_ref[0] = cx.astype(BF16)
        tailb_ref[0] = cx[rows - HALO_B:, :]

    @pl.when(j == 2)
    def _():
        p_ref[0] = z1().astype(BF16)

    @pl.when(j >= 3)
    def _():
        p_ref[0] = _sigmoid(z1()).astype(BF16)


def _inproj(x, g, w_in, b_in, rows):
    b, s, d = x.shape
    return pl.pallas_call(
        functools.partial(_inproj_kernel, rows=rows),
        grid=(b, s // rows, N_IN_STEPS),
        in_specs=[
            pl.BlockSpec((1, rows, d), lambda bi, i, j: (bi, i, 0)),
            pl.BlockSpec((1, d), lambda bi, i, j: (0, 0)),
            pl.BlockSpec((d, IN_BLOCK), lambda bi, i, j: (0, _w1_block(j))),
            pl.BlockSpec((d, IN_BLOCK), lambda bi, i, j: (0, _w2_block(j))),
            pl.BlockSpec((1, IN_BLOCK), lambda bi, i, j: (0, _w1_block(j))),
            pl.BlockSpec((1, IN_BLOCK), lambda bi, i, j: (0, _w2_block(j))),
        ],
        out_specs=[
            pl.BlockSpec((1, rows, IN_BLOCK), lambda bi, i, j: (bi, i, _p_block(j))),
            pl.BlockSpec((1, HALO_A, IN_BLOCK), lambda bi, i, j: (bi, 0, 0)),
            pl.BlockSpec((1, HALO_B, IN_BLOCK), lambda bi, i, j: (bi, 0, 0)),
        ],
        out_shape=[
            jax.ShapeDtypeStruct((b, s, N_IN_STEPS * IN_BLOCK), BF16),
            jax.ShapeDtypeStruct((b, HALO_A, IN_BLOCK), F32),
            jax.ShapeDtypeStruct((b, HALO_B, IN_BLOCK), F32),
        ],
        scratch_shapes=[pltpu.VMEM((rows, d), BF16)],
        compiler_params=_params(3),
        name="inproj",
    )(x, g, w_in, w_in, b_in, b_in)


def _mixer_kernel(a_ref, ah_ref, c_ref, ch_ref, bb_ref, sga_ref, sgb_ref, ctxa_ref, ctxb_ref, x_ref,
                  caw_ref, cab_ref, lng_ref, lnb_ref, wa_ref, ba_ref, cbw_ref, wb_ref, wm_ref,
                  o_ref, full_ref, rot_ref, acta_ref, fullb_ref, *, rows):
    first = pl.program_id(1) == 0
    d_conv = full_ref.shape[1]

    full_ref[0:HALO_A, :] = jnp.where(first, ctxa_ref[0], ah_ref[0].astype(F32))
    full_ref[HALO_A:, :] = a_ref[0].astype(F32)
    fullb_ref[0:HALO_B, :] = jnp.where(first, ctxb_ref[0], ch_ref[0].astype(F32))
    fullb_ref[HALO_B:, :] = c_ref[0].astype(F32)

    span = rot_ref.shape[1]
    for r in range(1, SUBLANES_F32):
        rot_ref[r - 1] = full_ref[pl.ds(r, span), :]

    crow = min(CONV_ROWS, rows)
    tap0 = HALO_A - (CONV_A_WIDTH - 1)

    def conv_body(rc, carry):
        r0 = pl.multiple_of(rc * crow, crow)
        parts = []
        for lb in range(d_conv // CONV_LANES):
            ls = slice(lb * CONV_LANES, (lb + 1) * CONV_LANES)
            acc = jnp.zeros((crow, CONV_LANES), F32)
            for k in range(CONV_A_WIDTH):
                q, r = divmod(tap0 + k, SUBLANES_F32)
                start = pl.multiple_of(r0 + q * SUBLANES_F32, SUBLANES_F32)
                if r == 0:
                    src = full_ref[pl.ds(start, crow), ls]
                else:
                    src = rot_ref[r - 1, pl.ds(start, crow), ls]
                acc = acc + src * caw_ref[k:k + 1, ls]
            parts.append(acc + cab_ref[:, ls])
        v = jnp.concatenate(parts, axis=-1)
        mu = jnp.mean(v, axis=-1, keepdims=True)
        vc = v - mu
        var = jnp.mean(vc * vc, axis=-1, keepdims=True)
        y = vc * lax.rsqrt(var + LN_EPS) * lng_ref[...] + lnb_ref[...]
        acta_ref[pl.ds(r0, crow), :] = (y * _sigmoid(y)).astype(BF16)
        return carry

    lax.fori_loop(0, rows // crow, conv_body, 0)
    a_out = _dot(acta_ref[...], wa_ref[...]) + ba_ref[...]

    tapb = HALO_B - (CONV_B_WIDTH - 1)
    cb = fullb_ref[pl.ds(tapb, rows), :] * cbw_ref[0:1, :]
    for k in range(1, CONV_B_WIDTH):
        cb = cb + fullb_ref[pl.ds(tapb + k, rows), :] * cbw_ref[k:k + 1, :]
    b_out = _dot((bb_ref[0].astype(F32) * cb).astype(BF16), wb_ref[...])

    m = sga_ref[0].astype(F32) * a_out + sgb_ref[0].astype(F32) * b_out
    o_ref[0] = x_ref[0] + _dot(m.astype(BF16), wm_ref[...])


def _mixer(p, ctx_a, ctx_b, x, caw, cab, lng, lnb, wa, ba, cbw, wb, wm, rows):
    b, s, d = x.shape
    d_conv = IN_BLOCK
    tile = lambda blk: pl.BlockSpec((1, rows, d_conv), lambda bi, i: (bi, i, blk))
    halo_a = pl.BlockSpec((1, HALO_A, d_conv),
                          lambda bi, i: (bi, jnp.maximum(i * (rows // HALO_A) - 1, 0), P_A))
    halo_b = pl.BlockSpec((1, HALO_B, d_conv),
                          lambda bi, i: (bi, jnp.maximum(i * (rows // HALO_B) - 1, 0), P_CX))
    gate = lambda blk: pl.BlockSpec((1, rows, d), lambda bi, i: (bi, i, blk))
    return pl.pallas_call(
        functools.partial(_mixer_kernel, rows=rows),
        grid=(b, s // rows),
        in_specs=[
            tile(P_A), halo_a, tile(P_CX), halo_b, tile(P_BB),
            gate(P_SGA // 2), gate(P_SGB // 2),
            pl.BlockSpec((1, HALO_A, d_conv), lambda bi, i: (bi, 0, 0)),
            pl.BlockSpec((1, HALO_B, d_conv), lambda bi, i: (bi, 0, 0)),
            pl.BlockSpec((1, rows, d), lambda bi, i: (bi, i, 0)),
            _resident(caw.shape), _resident(cab.shape), _resident(lng.shape), _resident(lnb.shape),
            _resident(wa.shape), _resident(ba.shape), _resident(cbw.shape), _resident(wb.shape),
            _resident(wm.shape),
        ],
        out_specs=pl.BlockSpec((1, rows, d), lambda bi, i: (bi, i, 0)),
        out_shape=jax.ShapeDtypeStruct((b, s, d), F32),
        scratch_shapes=[
            pltpu.VMEM((HALO_A + rows, d_conv), F32),
            pltpu.VMEM((SUBLANES_F32 - 1, HALO_A + rows - SUBLANES_F32, d_conv), F32),
            pltpu.VMEM((rows, d_conv), BF16),
            pltpu.VMEM((HALO_B + rows, d_conv), F32),
        ],
        compiler_params=_params(2),
        name="mixer",
    )(p, p, p, p, p, p, p, ctx_a, ctx_b, x, caw, cab, lng, lnb, wa, ba, cbw, wb, wm)


def _attn_kernel(x_ref, g_ref, wq_ref, k_ref, v_ref, wo_ref, o_ref, h_ref, q_ref, att_ref, *, rows):
    d = x_ref.shape[2]
    hd = d // N_XHEADS
    _norm_rows(h_ref, x_ref, g_ref, rows)
    q_ref[...] = _dot(h_ref[...], wq_ref[...]).astype(BF16)
    for h in range(N_XHEADS):
        hs = slice(h * hd, (h + 1) * hd)
        s = lax.dot_general(q_ref[:, hs], k_ref[0, :, hs], (((1,), (1,)), ((), ())),
                            preferred_element_type=F32) * (hd ** -0.5)
        e = jnp.exp(s - jnp.max(s, axis=-1, keepdims=True))
        inv = 1.0 / jnp.sum(e, axis=-1, keepdims=True)
        att_ref[:, hs] = (_dot(e.astype(BF16), v_ref[0, :, hs]) * inv).astype(BF16)
    o_ref[0] = x_ref[0] + _dot(att_ref[...], wo_ref[...])


def _attn(x, g, wq, k, v, wo, rows):
    b, s, d = x.shape
    n_mem = k.shape[1]
    return pl.pallas_call(
        functools.partial(_attn_kernel, rows=rows),
        grid=(b, s // rows),
        in_specs=[
            pl.BlockSpec((1, rows, d), lambda bi, i: (bi, i, 0)),
            _resident(g.shape), _resident(wq.shape),
            pl.BlockSpec((1, n_mem, d), lambda bi, i: (bi, 0, 0)),
            pl.BlockSpec((1, n_mem, d), lambda bi, i: (bi, 0, 0)),
            _resident(wo.shape),
        ],
        out_specs=pl.BlockSpec((1, rows, d), lambda bi, i: (bi, i, 0)),
        out_shape=jax.ShapeDtypeStruct((b, s, d), F32),
        scratch_shapes=[pltpu.VMEM((rows, d), BF16)] * 3,
        compiler_params=_params(2),
        name="attn",
    )(x, g, wq, k, v, wo)


def _ffn_kernel(x_ref, g_ref, wg_ref, wu_ref, wd_ref, gf_ref, o_ref, h_ref, acc_ref, *, rows):
    j = pl.program_id(2)

    @pl.when(j == 0)
    def _():
        _norm_rows(h_ref, x_ref, g_ref, rows)
        acc_ref[...] = jnp.zeros_like(acc_ref)

    h = h_ref[...]
    gate = _dot(h, wg_ref[...])
    act = (gate * _sigmoid(gate) * _dot(h, wu_ref[...])).astype(BF16)
    acc_ref[...] += _dot(act, wd_ref[...])

    @pl.when(j == pl.num_programs(2) - 1)
    def _():
        chunk = min(NORM_CHUNK, rows)

        def body(r, c):
            r0 = pl.multiple_of(r * chunk, chunk)
            y = x_ref[0, pl.ds(r0, chunk), :] + acc_ref[pl.ds(r0, chunk), :]
            o_ref[0, pl.ds(r0, chunk), :] = _rms(y, gf_ref[...])
            return c

        lax.fori_loop(0, rows // chunk, body, 0)


def _ffn(x, g, wg, wu, wd, gf, rows):
    b, s, d = x.shape
    d_ff = wg.shape[1]
    return pl.pallas_call(
        functools.partial(_ffn_kernel, rows=rows),
        grid=(b, s // rows, d_ff // FF_BLOCK),
        in_specs=[
            pl.BlockSpec((1, rows, d), lambda bi, i, j: (bi, i, 0)),
            pl.BlockSpec((1, d), lambda bi, i, j: (0, 0)),
            pl.BlockSpec((d, FF_BLOCK), lambda bi, i, j: (0, j)),
            pl.BlockSpec((d, FF_BLOCK), lambda bi, i, j: (0, j)),
            pl.BlockSpec((FF_BLOCK, d), lambda bi, i, j: (j, 0)),
            pl.BlockSpec((1, d), lambda bi, i, j: (0, 0)),
        ],
        out_specs=pl.BlockSpec((1, rows, d), lambda bi, i, j: (bi, i, 0)),
        out_shape=jax.ShapeDtypeStruct((b, s, d), F32),
        scratch_shapes=[pltpu.VMEM((rows, d), BF16), pltpu.VMEM((rows, d), F32)],
        compiler_params=_params(3),
        name="ffn",
    )(x, g, wg, wu, wd, gf)


def _pad_rows_front(a, rows):
    return jnp.pad(a, ((0, 0), (rows - a.shape[1], 0), (0, 0)))


def _layer(x, ctx_a, ctx_b, k, v, w, tiles):
    t_in, t_mix, t_attn, t_ffn = tiles
    p, tail_a, tail_b = _inproj(x, w["norm_mix_g"], w["w_in"], w["b_in"], t_in)
    x = _mixer(p, _pad_rows_front(ctx_a, HALO_A), _pad_rows_front(ctx_b, HALO_B), x,
               w["conv_a_w"], w["conv_a_b"], w["ln_a_g"], w["ln_a_b"], w["w_a_out"], w["b_a_out"],
               w["conv_b_w"], w["w_b_out"], w["w_mix_out"], t_mix)
    x = _attn(x, w["norm_x_g"], w["w_q"], k, v, w["w_o"], t_attn)
    y = _ffn(x, w["norm_ffn_g"], w["w_gate"], w["w_up"], w["w_down"], w["norm_final_g"], t_ffn)
    new_a = tail_a[:, HALO_A - (CONV_A_WIDTH - 1):, :]
    new_b = tail_b[:, HALO_B - (CONV_B_WIDTH - 1):, :]
    return y, new_a, new_b


def kernel(x_prompt, x_sample, state_conv_a, state_conv_b, cache_mem_k, cache_mem_v, mem_prompt,
           norm_mix_g, w_in, b_in, conv_a_w, conv_a_b, ln_a_g, ln_a_b, w_a_out, b_a_out,
           conv_b_w, w_b_out, w_mix_out, norm_x_g, norm_mem_g, w_q, w_k, w_v, w_o,
           norm_ffn_g, w_gate, w_up, w_down, norm_final_g):
    assert norm_mix_g.shape[0] == 1, "single-layer trunk"
    row = lambda a: a.reshape(1, -1)
    w = {
        "norm_mix_g": row(norm_mix_g[0]), "w_in": w_in[0].astype(BF16), "b_in": row(b_in[0]),
        "conv_a_w": conv_a_w[0], "conv_a_b": row(conv_a_b[0]),
        "ln_a_g": row(ln_a_g[0]), "ln_a_b": row(ln_a_b[0]),
        "w_a_out": w_a_out[0].astype(BF16), "b_a_out": row(b_a_out[0]),
        "conv_b_w": conv_b_w[0], "w_b_out": w_b_out[0].astype(BF16),
        "w_mix_out": w_mix_out[0].astype(BF16),
        "norm_x_g": row(norm_x_g[0]), "w_q": w_q[0].astype(BF16), "w_o": w_o[0].astype(BF16),
        "norm_ffn_g": row(norm_ffn_g[0]), "w_gate": w_gate[0].astype(BF16),
        "w_up": w_up[0].astype(BF16), "w_down": w_down[0].astype(BF16),
        "norm_final_g": row(norm_final_g),
    }
    bp, _, d = x_prompt.shape
    bs, s_len, _ = x_sample.shape
    d_conv = conv_a_w.shape[-1]
    hd = d // N_XHEADS

    kp, vp = _memkv(mem_prompt, row(norm_mem_g[0]), w_k[0].astype(BF16), w_v[0].astype(BF16))
    zero_a = jnp.zeros((bp, CONV_A_WIDTH - 1, d_conv), F32)
    zero_b = jnp.zeros((bp, CONV_B_WIDTH - 1, d_conv), F32)
    y_p, na_p, nb_p = _layer(x_prompt, zero_a, zero_b, kp.astype(BF16), vp.astype(BF16), w,
                             (512, 256, 512, 512))

    n_mem = cache_mem_k.shape[2]
    ks = cache_mem_k[0].reshape(bs, n_mem, d).astype(BF16)
    vs = cache_mem_v[0].reshape(bs, n_mem, d).astype(BF16)
    y_s, na_s, nb_s = _layer(x_sample, state_conv_a[0], state_conv_b[0], ks, vs, w,
                             (s_len,) * 4)

    kv_shape = (1, bp, n_mem, N_XHEADS, hd)
    return (y_p, y_s, na_p[None], nb_p[None], kp.reshape(kv_shape), vp.reshape(kv_shape),
            na_s[None], nb_s[None])
```

```python
import functools

import jax
import jax.numpy as jnp
from jax import lax
from jax.experimental import pallas as pl
from jax.experimental.pallas import tpu as pltpu

F32 = jnp.float32
BF16 = jnp.bfloat16

RMS_EPS = 1e-6
LN_EPS = 1e-5
N_XHEADS = 4
CONV_A_WIDTH = 31
CONV_B_WIDTH = 3

LANES = 128
SUBLANES_F32 = 8
SUBLANES_BF16 = 16
V7X_VMEM_LIMIT_BYTES = 56 * 1024 * 1024

HALO_A = 32
HALO_B = 16
assert HALO_A >= CONV_A_WIDTH - 1 and HALO_A % SUBLANES_F32 == 0
assert HALO_B >= CONV_B_WIDTH - 1 and HALO_B % SUBLANES_F32 == 0

NORM_CHUNK = 16
NORM_UNROLL = 8
CONV_ROWS = 64
IN_BLOCK = 1024
CONV_B_LANES = 256
FF_BLOCK = 512
KV_BLOCK = 512


def _sigmoid(x):
    return 0.5 * jnp.tanh(0.5 * x) + 0.5


def _rms(x, g):
    ms = jnp.mean(x * x, axis=-1, keepdims=True)
    return x * lax.rsqrt(ms + RMS_EPS) * g


def _dot(a, b):
    return jnp.dot(a, b, preferred_element_type=F32)


def _norm_rows(dst_ref, src_ref, g_ref, rows):
    chunk = min(NORM_CHUNK, rows)

    def body(r, c):
        r0 = pl.multiple_of(r * chunk, chunk)
        dst_ref[pl.ds(r0, chunk), :] = _rms(src_ref[0, pl.ds(r0, chunk), :], g_ref[...]).astype(BF16)
        return c

    n = rows // chunk
    lax.fori_loop(0, n, body, 0, unroll=min(NORM_UNROLL, n))


def _params(n_grid_dims):
    return pltpu.CompilerParams(
        dimension_semantics=("arbitrary",) * n_grid_dims,
        vmem_limit_bytes=V7X_VMEM_LIMIT_BYTES,
    )


def _resident(shape):
    return pl.BlockSpec(shape, lambda *_: (0,) * len(shape), pipeline_mode=pl.Buffered(1))


def _memkv_kernel(m_ref, g_ref, wk_ref, wv_ref, k_ref, v_ref, h_ref, *, rows):
    @pl.when(pl.program_id(1) == 0)
    def _():
        _norm_rows(h_ref, m_ref, g_ref, rows)

    h = h_ref[...]
    k_ref[0] = _dot(h, wk_ref[...])
    v_ref[0] = _dot(h, wv_ref[...])


def _memkv(mem, g, wk, wv):
    b, n, d = mem.shape
    out = jax.ShapeDtypeStruct((b, n, d), F32)
    return pl.pallas_call(
        functools.partial(_memkv_kernel, rows=n),
        grid=(b, d // KV_BLOCK),
        in_specs=[
            pl.BlockSpec((1, n, d), lambda i, j: (i, 0, 0)),
            pl.BlockSpec((1, d), lambda i, j: (0, 0)),
            pl.BlockSpec((d, KV_BLOCK), lambda i, j: (0, j)),
            pl.BlockSpec((d, KV_BLOCK), lambda i, j: (0, j)),
        ],
        out_specs=[
            pl.BlockSpec((1, n, KV_BLOCK), lambda i, j: (i, 0, j)),
            pl.BlockSpec((1, n, KV_BLOCK), lambda i, j: (i, 0, j)),
        ],
        out_shape=[out, out],
        scratch_shapes=[pltpu.VMEM((n, d), BF16)],
        compiler_params=_params(2),
        name="memkv",
    )(mem, g, wk, wv)


N_IN_STEPS = 9
STEP_A_GATE, STEP_BB, STEP_BX, STEP_GATES = 1, 2, 4, 5
N_GATE_BLOCKS = N_IN_STEPS - STEP_GATES
CONV_A_SHARES = {2: 2, 3: 2, 5: 1, 6: 1, 7: 1, 8: 1}
CONV_B_SHARES = {5: 2, 6: 2, 7: 2, 8: 2}
RES_MAIN, RES_BB = 0, 1


def _conv_a_taps_by_phase():
    tap0 = HALO_A - (CONV_A_WIDTH - 1)
    phases = {}
    for k in range(CONV_A_WIDTH):
        q, r = divmod(tap0 + k, SUBLANES_F32)
        phases.setdefault(r, []).append((q, k))
    return sorted(phases.items())


def _share_bounds(j, shares, per_share):
    lo = jnp.int32(0)
    hi = jnp.int32(0)
    start = 0
    for step, n in sorted(shares.items()):
        lo = jnp.where(j == step, start * per_share, lo)
        hi = jnp.where(j == step, (start + n) * per_share, hi)
        start += n
    return lo, hi


def _inproj_kernel(x_ref, g_ref, w_ref, b_ref, ctxa_ref, ctxb_ref,
                   caw_ref, cab_ref, lng_ref, lnb_ref, cbw_ref,
                   gates_ref, act_ref, taila_ref, tailb_ref,
                   u_ref, kept_ref, res_ref, fulla_ref, fullb_ref, conv_ref, rot_ref, *, rows, nseg):
    i = pl.program_id(1)
    j = pl.program_id(2)
    seg = rows // nseg
    d_conv = fulla_ref.shape[1]
    crow = min(CONV_ROWS, seg)
    per_share = (rows // crow) // 8
    stride_a = HALO_A + seg
    stride_b = HALO_B + seg
    rot_rows = rot_ref.shape[1]
    phases = _conv_a_taps_by_phase()

    def load_history(full_ref, ctx_ref, halo, stride):
        if nseg == 1:
            @pl.when(i == 0)
            def _():
                full_ref[0:halo, :] = ctx_ref[0]

            @pl.when(i > 0)
            def _():
                full_ref[0:halo, :] = full_ref[seg:seg + halo, :]
        else:
            for sg in range(nseg):
                full_ref[sg * stride:sg * stride + halo, :] = ctx_ref[sg]

    def store_rows(full_ref, tail_ref, halo, stride):
        for sg in range(nseg):
            full_ref[sg * stride + halo:(sg + 1) * stride, :] = res_ref[RES_MAIN, sg * seg:(sg + 1) * seg, :]
            tail_ref[sg] = res_ref[RES_MAIN, (sg + 1) * seg - halo:(sg + 1) * seg, :]

    def item_base(c, stride):
        return pl.multiple_of(c * (crow if nseg == 1 else stride), SUBLANES_F32)

    def conv_a_item(c, carry):
        base = item_base(c, stride_a)
        out0 = pl.multiple_of(c * crow, crow)
        for lb in range(d_conv // LANES):
            ls = slice(lb * LANES, (lb + 1) * LANES)
            blk = fulla_ref[pl.ds(base, crow + HALO_A), ls]
            for r, _ in phases:
                if r:
                    rot_ref[r - 1, :, ls] = blk[r:r + rot_rows]
            acc = jnp.zeros((crow, LANES), F32)
            for r, taps in phases:
                for q, k in taps:
                    lo = q * SUBLANES_F32
                    src = rot_ref[r - 1, lo:lo + crow, ls] if r else blk[lo:lo + crow]
                    acc = acc + src * caw_ref[k:k + 1, ls]
            conv_ref[pl.ds(out0, crow), ls] = acc + cab_ref[:, ls]
        half = min(crow, 32)
        for h in range(crow // half):
            h0 = pl.multiple_of(out0 + h * half, half)
            v = conv_ref[pl.ds(h0, half), :]
            mu = jnp.mean(v, axis=-1, keepdims=True)
            vc = v - mu
            var = jnp.mean(vc * vc, axis=-1, keepdims=True)
            y = vc * lax.rsqrt(var + LN_EPS) * lng_ref[...] + lnb_ref[...]
            act_ref[0, pl.ds(h0, half), 0:d_conv] = (y * _sigmoid(y)).astype(BF16)
        return carry

    def conv_b_item(c, carry):
        base = item_base(c, stride_b) + HALO_B - SUBLANES_F32
        out0 = pl.multiple_of(c * crow, crow)
        tap0 = SUBLANES_F32 - (CONV_B_WIDTH - 1)
        for lb in range(d_conv // CONV_B_LANES):
            ls = slice(lb * CONV_B_LANES, (lb + 1) * CONV_B_LANES)
            blk = fullb_ref[pl.ds(base, crow + SUBLANES_F32), ls]
            cb = blk[tap0:tap0 + crow] * cbw_ref[0:1, ls]
            for k in range(1, CONV_B_WIDTH):
                cb = cb + blk[tap0 + k:tap0 + k + crow] * cbw_ref[k:k + 1, ls]
            act_ref[0, pl.ds(out0, crow), d_conv + lb * CONV_B_LANES:d_conv + (lb + 1) * CONV_B_LANES] = (
                res_ref[RES_BB, pl.ds(out0, crow), ls] * cb).astype(BF16)
        return carry

    @pl.when(j == 0)
    def _():
        _norm_rows(u_ref, x_ref, g_ref, rows)
        load_history(fulla_ref, ctxa_ref, HALO_A, stride_a)
        load_history(fullb_ref, ctxb_ref, HALO_B, stride_b)
        kept_ref[...] = jnp.zeros_like(kept_ref)

    is_glu = j == STEP_A_GATE
    is_prod = j == STEP_BX
    is_gate = j >= STEP_GATES
    keeps = jnp.logical_not(is_glu | is_prod | is_gate | (j == STEP_BB))
    z = _dot(u_ref[...], w_ref[...]) + b_ref[...]
    kept = kept_ref[...]
    sg = _sigmoid(z)
    res = jnp.where(is_glu | is_prod, kept, 1.0) * jnp.where(is_glu | is_gate, sg, z)
    kept_ref[...] = jnp.where(keeps, z, kept)
    res_ref[jnp.where(j == STEP_BB, RES_BB, RES_MAIN)] = res
    gates_ref[0] = res.astype(BF16)

    @pl.when(is_glu)
    def _():
        store_rows(fulla_ref, taila_ref, HALO_A, stride_a)

    @pl.when(is_prod)
    def _():
        store_rows(fullb_ref, tailb_ref, HALO_B, stride_b)

    a_lo, a_hi = _share_bounds(j, CONV_A_SHARES, per_share)
    b_lo, b_hi = _share_bounds(j, CONV_B_SHARES, per_share)

    @pl.when(a_hi > a_lo)
    def _():
        lax.fori_loop(a_lo, a_hi, conv_a_item, 0)

    @pl.when(b_hi > b_lo)
    def _():
        lax.fori_loop(b_lo, b_hi, conv_b_item, 0)


def _inproj(x, g, w_in, b_in, ctx_a, ctx_b, caw, cab, lng, lnb, cbw, rows, nseg):
    b, s, d = x.shape
    d_conv = IN_BLOCK
    seg = rows // nseg
    crow = min(CONV_ROWS, seg)
    assert nseg == 1 or (s == rows and crow == seg), "stacked segments are whole sequences, one item each"
    assert seg >= HALO_A and (rows // crow) % 8 == 0
    assert min(CONV_A_SHARES) > STEP_A_GATE and min(CONV_B_SHARES) > STEP_BX
    return pl.pallas_call(
        functools.partial(_inproj_kernel, rows=rows, nseg=nseg),
        grid=(b, s // rows, N_IN_STEPS),
        in_specs=[
            pl.BlockSpec((1, rows, d), lambda bi, i, j: (bi, i, 0)),
            pl.BlockSpec((1, d), lambda bi, i, j: (0, 0)),
            pl.BlockSpec((d, IN_BLOCK), lambda bi, i, j: (0, j)),
            pl.BlockSpec((1, IN_BLOCK), lambda bi, i, j: (0, j)),
            pl.BlockSpec((nseg, HALO_A, d_conv), lambda bi, i, j: (bi, 0, 0)),
            pl.BlockSpec((nseg, HALO_B, d_conv), lambda bi, i, j: (bi, 0, 0)),
            _resident(caw.shape), _resident(cab.shape), _resident(lng.shape), _resident(lnb.shape),
            _resident(cbw.shape),
        ],
        out_specs=[
            pl.BlockSpec((1, rows, IN_BLOCK), lambda bi, i, j: (bi, i, jnp.maximum(j - STEP_GATES, 0))),
            pl.BlockSpec((1, rows, 2 * d_conv), lambda bi, i, j: (bi, i, 0)),
            pl.BlockSpec((nseg, HALO_A, d_conv), lambda bi, i, j: (bi, 0, 0)),
            pl.BlockSpec((nseg, HALO_B, d_conv), lambda bi, i, j: (bi, 0, 0)),
        ],
        out_shape=[
            jax.ShapeDtypeStruct((b, s, N_GATE_BLOCKS * IN_BLOCK), BF16),
            jax.ShapeDtypeStruct((b, s, 2 * d_conv), BF16),
            jax.ShapeDtypeStruct((b * nseg, HALO_A, d_conv), F32),
            jax.ShapeDtypeStruct((b * nseg, HALO_B, d_conv), F32),
        ],
        scratch_shapes=[
            pltpu.VMEM((rows, d), BF16),
            pltpu.VMEM((rows, d_conv), F32),
            pltpu.VMEM((2, rows, d_conv), F32),
            pltpu.VMEM((nseg * (HALO_A + seg), d_conv), F32),
            pltpu.VMEM((nseg * (HALO_B + seg), d_conv), F32),
            pltpu.VMEM((rows, d_conv), F32),
            pltpu.VMEM((SUBLANES_F32 - 1, crow + HALO_A - SUBLANES_F32, d_conv), F32),
        ],
        compiler_params=_params(3),
        name="inproj",
    )(x, g, w_in, b_in, ctx_a, ctx_b, caw, cab, lng, lnb, cbw)


def _mixer_kernel(act_ref, sga_ref, sgb_ref, x_ref, wa_ref, ba_ref, wb_ref, wm_ref, o_ref):
    d_conv = wa_ref.shape[0]
    a_out = _dot(act_ref[0, :, 0:d_conv], wa_ref[...]) + ba_ref[...]
    b_out = _dot(act_ref[0, :, d_conv:2 * d_conv], wb_ref[...])
    m = sga_ref[0].astype(F32) * a_out + sgb_ref[0].astype(F32) * b_out
    o_ref[0] = x_ref[0] + _dot(m.astype(BF16), wm_ref[...])


def _mixer(act, gates, x, wa, ba, wb, wm, rows):
    b, s, d = x.shape
    return pl.pallas_call(
        _mixer_kernel,
        grid=(b, s // rows),
        in_specs=[
            pl.BlockSpec((1, rows, act.shape[2]), lambda bi, i: (bi, i, 0)),
            pl.BlockSpec((1, rows, d), lambda bi, i: (bi, i, 0)),
            pl.BlockSpec((1, rows, d), lambda bi, i: (bi, i, 1)),
            pl.BlockSpec((1, rows, d), lambda bi, i: (bi, i, 0)),
            _resident(wa.shape), _resident(ba.shape), _resident(wb.shape), _resident(wm.shape),
        ],
        out_specs=pl.BlockSpec((1, rows, d), lambda bi, i: (bi, i, 0)),
        out_shape=jax.ShapeDtypeStruct((b, s, d), F32),
        compiler_params=_params(2),
        name="mixer",
    )(act, gates, gates, x, wa, ba, wb, wm)


def _attn_kernel(x_ref, g_ref, wq_ref, k_ref, v_ref, wo_ref, o_ref, h_ref, q_ref, att_ref, *, rows):
    d = x_ref.shape[2]
    hd = d // N_XHEADS
    _norm_rows(h_ref, x_ref, g_ref, rows)
    q_ref[...] = _dot(h_ref[...], wq_ref[...]).astype(BF16)
    for h in range(N_XHEADS):
        hs = slice(h * hd, (h + 1) * hd)
        s = lax.dot_general(q_ref[:, hs], k_ref[0, :, hs], (((1,), (1,)), ((), ())),
                            preferred_element_type=F32) * (hd ** -0.5)
        e = jnp.exp(s - jnp.max(s, axis=-1, keepdims=True))
        inv = 1.0 / jnp.sum(e, axis=-1, keepdims=True)
        att_ref[:, hs] = (_dot(e.astype(BF16), v_ref[0, :, hs]) * inv).astype(BF16)
    o_ref[0] = x_ref[0] + _dot(att_ref[...], wo_ref[...])


def _attn(x, g, wq, k, v, wo, rows):
    b, s, d = x.shape
    n_mem = k.shape[1]
    return pl.pallas_call(
        functools.partial(_attn_kernel, rows=rows),
        grid=(b, s // rows),
        in_specs=[
            pl.BlockSpec((1, rows, d), lambda bi, i: (bi, i, 0)),
            _resident(g.shape), _resident(wq.shape),
            pl.BlockSpec((1, n_mem, d), lambda bi, i: (bi, 0, 0)),
            pl.BlockSpec((1, n_mem, d), lambda bi, i: (bi, 0, 0)),
            _resident(wo.shape),
        ],
        out_specs=pl.BlockSpec((1, rows, d), lambda bi, i: (bi, i, 0)),
        out_shape=jax.ShapeDtypeStruct((b, s, d), F32),
        scratch_shapes=[pltpu.VMEM((rows, d), BF16)] * 3,
        compiler_params=_params(2),
        name="attn",
    )(x, g, wq, k, v, wo)


def _ffn_kernel(x_ref, g_ref, wg_ref, wu_ref, wd_ref, gf_ref, o_ref, h_ref, acc_ref, *, rows):
    j = pl.program_id(2)

    @pl.when(j == 0)
    def _():
        _norm_rows(h_ref, x_ref, g_ref, rows)
        acc_ref[...] = jnp.zeros_like(acc_ref)

    h = h_ref[...]
    gate = _dot(h, wg_ref[...])
    act = (gate * _sigmoid(gate) * _dot(h, wu_ref[...])).astype(BF16)
    acc_ref[...] += _dot(act, wd_ref[...])

    @pl.when(j == pl.num_programs(2) - 1)
    def _():
        chunk = min(NORM_CHUNK, rows)

        def body(r, c):
            r0 = pl.multiple_of(r * chunk, chunk)
            y = x_ref[0, pl.ds(r0, chunk), :] + acc_ref[pl.ds(r0, chunk), :]
            o_ref[0, pl.ds(r0, chunk), :] = _rms(y, gf_ref[...])
            return c

        n = rows // chunk
        lax.fori_loop(0, n, body, 0, unroll=min(NORM_UNROLL, n))


def _ffn(x, g, wg, wu, wd, gf, rows):
    b, s, d = x.shape
    d_ff = wg.shape[1]
    return pl.pallas_call(
        functools.partial(_ffn_kernel, rows=rows),
        grid=(b, s // rows, d_ff // FF_BLOCK),
        in_specs=[
            pl.BlockSpec((1, rows, d), lambda bi, i, j: (bi, i, 0)),
            pl.BlockSpec((1, d), lambda bi, i, j: (0, 0)),
            pl.BlockSpec((d, FF_BLOCK), lambda bi, i, j: (0, j)),
            pl.BlockSpec((d, FF_BLOCK), lambda bi, i, j: (0, j)),
            pl.BlockSpec((FF_BLOCK, d), lambda bi, i, j: (j, 0)),
            pl.BlockSpec((1, d), lambda bi, i, j: (0, 0)),
        ],
        out_specs=pl.BlockSpec((1, rows, d), lambda bi, i, j: (bi, i, 0)),
        out_shape=jax.ShapeDtypeStruct((b, s, d), F32),
        scratch_shapes=[pltpu.VMEM((rows, d), BF16), pltpu.VMEM((rows, d), F32)],
        compiler_params=_params(3),
        name="ffn",
    )(x, g, wg, wu, wd, gf)


def _pad_rows_front(a, rows):
    return jnp.pad(a, ((0, 0), (rows - a.shape[1], 0), (0, 0)))


def _layer(x, ctx_a, ctx_b, k, v, w, stack, tiles):
    b, s, d = x.shape
    t_in, t_mix, t_attn, t_ffn = tiles
    ctx_a = _pad_rows_front(ctx_a, HALO_A)
    ctx_b = _pad_rows_front(ctx_b, HALO_B)
    xs = x.reshape(1, b * s, d) if stack else x
    nseg = b if stack else 1
    gates, act, tail_a, tail_b = _inproj(
        xs, w["norm_mix_g"], w["w_in"], w["b_in"], ctx_a, ctx_b,
        w["conv_a_w"], w["conv_a_b"], w["ln_a_g"], w["ln_a_b"], w["conv_b_w"], t_in, nseg)
    xs = _mixer(act, gates, xs, w["w_a_out"], w["b_a_out"], w["w_b_out"], w["w_mix_out"], t_mix)
    xa = _attn(xs.reshape(b, s, d), w["norm_x_g"], w["w_q"], k, v, w["w_o"], t_attn)
    y = _ffn(xa.reshape(xs.shape), w["norm_ffn_g"], w["w_gate"], w["w_up"], w["w_down"],
             w["norm_final_g"], t_ffn)
    new_a = tail_a[:, HALO_A - (CONV_A_WIDTH - 1):, :]
    new_b = tail_b[:, HALO_B - (CONV_B_WIDTH - 1):, :]
    return y.reshape(b, s, d), new_a, new_b


def kernel(x_prompt, x_sample, state_conv_a, state_conv_b, cache_mem_k, cache_mem_v, mem_prompt,
           norm_mix_g, w_in, b_in, conv_a_w, conv_a_b, ln_a_g, ln_a_b, w_a_out, b_a_out,
           conv_b_w, w_b_out, w_mix_out, norm_x_g, norm_mem_g, w_q, w_k, w_v, w_o,
           norm_ffn_g, w_gate, w_up, w_down, norm_final_g):
    assert norm_mix_g.shape[0] == 1, "single-layer trunk"
    row = lambda a: a.reshape(1, -1)
    w = {
        "norm_mix_g": row(norm_mix_g[0]), "w_in": w_in[0].astype(BF16), "b_in": row(b_in[0]),
        "conv_a_w": conv_a_w[0], "conv_a_b": row(conv_a_b[0]),
        "ln_a_g": row(ln_a_g[0]), "ln_a_b": row(ln_a_b[0]),
        "w_a_out": w_a_out[0].astype(BF16), "b_a_out": row(b_a_out[0]),
        "conv_b_w": conv_b_w[0], "w_b_out": w_b_out[0].astype(BF16),
        "w_mix_out": w_mix_out[0].astype(BF16),
        "norm_x_g": row(norm_x_g[0]), "w_q": w_q[0].astype(BF16), "w_o": w_o[0].astype(BF16),
        "norm_ffn_g": row(norm_ffn_g[0]), "w_gate": w_gate[0].astype(BF16),
        "w_up": w_up[0].astype(BF16), "w_down": w_down[0].astype(BF16),
        "norm_final_g": row(norm_final_g),
    }
    bp, sp, d = x_prompt.shape
    bs, ss, _ = x_sample.shape
    d_conv = conv_a_w.shape[-1]
    hd = d // N_XHEADS

    kp, vp = _memkv(mem_prompt, row(norm_mem_g[0]), w_k[0].astype(BF16), w_v[0].astype(BF16))
    zero_a = jnp.zeros((bp, CONV_A_WIDTH - 1, d_conv), F32)
    zero_b = jnp.zeros((bp, CONV_B_WIDTH - 1, d_conv), F32)
    y_p, na_p, nb_p = _layer(x_prompt, zero_a, zero_b, kp.astype(BF16), vp.astype(BF16), w,
                             False, (512, 256, 512, 512))

    n_mem = cache_mem_k.shape[2]
    ks = cache_mem_k[0].reshape(bs, n_mem, d).astype(BF16)
    vs = cache_mem_v[0].reshape(bs, n_mem, d).astype(BF16)
    y_s, na_s, nb_s = _layer(x_sample, state_conv_a[0], state_conv_b[0], ks, vs, w,
                             True, (bs * ss, bs * ss, ss, bs * ss))

    kv_shape = (1, bp, n_mem, N_XHEADS, hd)
    return (y_p, y_s, na_p[None], nb_p[None], kp.reshape(kv_shape), vp.reshape(kv_shape),
            na_s[None], nb_s[None])
```

```python
import functools

import jax
import jax.numpy as jnp
from jax import lax
from jax.experimental import pallas as pl
from jax.experimental.pallas import tpu as pltpu

F32 = jnp.float32
BF16 = jnp.bfloat16

RMS_EPS = 1e-6
LN_EPS = 1e-5
N_XHEADS = 4
CONV_A_WIDTH = 31
CONV_B_WIDTH = 3

LANES = 128
SUBLANES_F32 = 8
SUBLANES_BF16 = 16
V7X_VMEM_LIMIT_BYTES = 56 * 1024 * 1024

HALO_A = 32
HALO_B = 16
assert HALO_A >= CONV_A_WIDTH - 1 and HALO_A % SUBLANES_F32 == 0
assert HALO_B >= CONV_B_WIDTH - 1 and HALO_B % SUBLANES_F32 == 0

NORM_CHUNK = 16
NORM_UNROLL = 8
CONV_ROWS = 64
IN_BLOCK = 1024
SUB_BLOCK = 1024
CONV_B_LANES = 256
FF_BLOCK = 512
KV_BLOCK = 512


def _sigmoid(x):
    return 0.5 * jnp.tanh(0.5 * x) + 0.5


def _rms(x, g):
    ms = jnp.mean(x * x, axis=-1, keepdims=True)
    return x * lax.rsqrt(ms + RMS_EPS) * g


def _dot(a, b):
    return jnp.dot(a, b, preferred_element_type=F32)


def _norm_rows(dst_ref, src_ref, g_ref, rows):
    chunk = min(NORM_CHUNK, rows)

    def body(r, c):
        r0 = pl.multiple_of(r * chunk, chunk)
        dst_ref[pl.ds(r0, chunk), :] = _rms(src_ref[0, pl.ds(r0, chunk), :], g_ref[...]).astype(BF16)
        return c

    n = rows // chunk
    lax.fori_loop(0, n, body, 0, unroll=min(NORM_UNROLL, n))


def _params(n_grid_dims):
    return pltpu.CompilerParams(
        dimension_semantics=("arbitrary",) * n_grid_dims,
        vmem_limit_bytes=V7X_VMEM_LIMIT_BYTES,
    )


def _resident(shape):
    return pl.BlockSpec(shape, lambda *_: (0,) * len(shape), pipeline_mode=pl.Buffered(1))


def _memkv_kernel(m_ref, g_ref, wk_ref, wv_ref, k_ref, v_ref, h_ref, *, rows):
    @pl.when(pl.program_id(1) == 0)
    def _():
        _norm_rows(h_ref, m_ref, g_ref, rows)

    h = h_ref[...]
    k_ref[0] = _dot(h, wk_ref[...])
    v_ref[0] = _dot(h, wv_ref[...])


def _memkv(mem, g, wk, wv):
    b, n, d = mem.shape
    out = jax.ShapeDtypeStruct((b, n, d), F32)
    return pl.pallas_call(
        functools.partial(_memkv_kernel, rows=n),
        grid=(b, d // KV_BLOCK),
        in_specs=[
            pl.BlockSpec((1, n, d), lambda i, j: (i, 0, 0)),
            pl.BlockSpec((1, d), lambda i, j: (0, 0)),
            pl.BlockSpec((d, KV_BLOCK), lambda i, j: (0, j)),
            pl.BlockSpec((d, KV_BLOCK), lambda i, j: (0, j)),
        ],
        out_specs=[
            pl.BlockSpec((1, n, KV_BLOCK), lambda i, j: (i, 0, j)),
            pl.BlockSpec((1, n, KV_BLOCK), lambda i, j: (i, 0, j)),
        ],
        out_shape=[out, out],
        scratch_shapes=[pltpu.VMEM((n, d), BF16)],
        compiler_params=_params(2),
        name="memkv",
    )(mem, g, wk, wv)


N_IN_STEPS = 7
FIRST_GATE_STEP = 3
N_GATE_BLOCKS = 4
CONV_A_STEP_WEIGHTS = (3, 1, 1, 1, 1, 1)
CONV_B_STEP_WEIGHTS = (0, 0, 1, 1, 1, 1)


def _w1_block(j):
    return jnp.where(j == 0, 0, jnp.where(j == 1, 3, jnp.where(j == 2, 2, j + 2)))


def _w2_block(j):
    return jnp.where(j == 0, 1, 4)


def _split_items(n_items, weights):
    total = sum(weights)
    bounds = [0]
    acc = 0
    for w in weights:
        acc += w
        bounds.append(-(-n_items * acc // total))
    return [range(bounds[k], bounds[k + 1]) for k in range(len(weights))]


def _conv_a_taps_by_phase():
    tap0 = HALO_A - (CONV_A_WIDTH - 1)
    phases = {}
    for k in range(CONV_A_WIDTH):
        q, r = divmod(tap0 + k, SUBLANES_F32)
        phases.setdefault(r, []).append((q, k))
    return phases


def _inproj_kernel(x_ref, g_ref, w1_ref, w2_ref, b1_ref, b2_ref, ctxa_ref, ctxb_ref,
                   caw_ref, cab_ref, lng_ref, lnb_ref, cbw_ref,
                   gates_ref, act_ref, taila_ref, tailb_ref,
                   u_ref, fulla_ref, fullb_ref, bb_ref, conv_ref, rot_ref, *, rows, nseg):
    i = pl.program_id(1)
    j = pl.program_id(2)
    seg = rows // nseg
    d_conv = fulla_ref.shape[1]
    crow = min(CONV_ROWS, seg)
    n_items = rows // crow
    stride_a = HALO_A + seg
    stride_b = HALO_B + seg

    def z(w_ref, b_ref, cols):
        return _dot(u_ref[...], w_ref[:, cols]) + b_ref[:, cols]

    def item_base(c, stride):
        return c * (crow if nseg == 1 else stride)

    def load_history(full_ref, ctx_ref, halo, stride):
        if nseg == 1:
            @pl.when(i == 0)
            def _():
                full_ref[0:halo, :] = ctx_ref[0]

            @pl.when(i > 0)
            def _():
                full_ref[0:halo, :] = full_ref[seg:seg + halo, :]
        else:
            for sg in range(nseg):
                full_ref[sg * stride:sg * stride + halo, :] = ctx_ref[sg]

    phases = sorted(_conv_a_taps_by_phase().items())
    rot_rows = rot_ref.shape[1]

    def aligned(v, m):
        return v if isinstance(v, int) else pl.multiple_of(v, m)

    def conv_a_pieces(c):
        base = aligned(item_base(c, stride_a), SUBLANES_F32)
        out0 = aligned(c * crow, crow)

        def taps_piece(ls):
            blk = fulla_ref[pl.ds(base, crow + HALO_A), ls]
            for r, _ in phases:
                if r:
                    rot_ref[r - 1, :, ls] = blk[r:r + rot_rows]
            acc = jnp.zeros((crow, LANES), F32)
            for r, taps in phases:
                for q, k in taps:
                    lo = q * SUBLANES_F32
                    src = rot_ref[r - 1, lo:lo + crow, ls] if r else blk[lo:lo + crow]
                    acc = acc + src * caw_ref[k:k + 1, ls]
            conv_ref[pl.ds(out0, crow), ls] = acc + cab_ref[:, ls]

        def norm_piece(h, n):
            h0 = aligned(out0 + h * n, n)
            v = conv_ref[pl.ds(h0, n), :]
            mu = jnp.mean(v, axis=-1, keepdims=True)
            vc = v - mu
            var = jnp.mean(vc * vc, axis=-1, keepdims=True)
            y = vc * lax.rsqrt(var + LN_EPS) * lng_ref[...] + lnb_ref[...]
            act_ref[0, pl.ds(h0, n), 0:d_conv] = (y * _sigmoid(y)).astype(BF16)

        pieces = [functools.partial(taps_piece, slice(lb * LANES, (lb + 1) * LANES))
                  for lb in range(d_conv // LANES)]
        half = min(crow, 32)
        pieces += [functools.partial(norm_piece, h, half) for h in range(crow // half)]
        return pieces

    def conv_b_pieces(c):
        base = aligned(item_base(c, stride_b) + HALO_B - SUBLANES_F32, SUBLANES_F32)
        out0 = aligned(c * crow, crow)
        tap0 = SUBLANES_F32 - (CONV_B_WIDTH - 1)

        def piece(ls):
            blk = fullb_ref[pl.ds(base, crow + SUBLANES_F32), ls]
            cb = blk[tap0:tap0 + crow] * cbw_ref[0:1, ls]
            for k in range(1, CONV_B_WIDTH):
                cb = cb + blk[tap0 + k:tap0 + k + crow] * cbw_ref[k:k + 1, ls]
            act_ref[0, pl.ds(out0, crow), d_conv + ls.start:d_conv + ls.stop] = (
                bb_ref[pl.ds(out0, crow), ls].astype(F32) * cb).astype(BF16)

        return [functools.partial(piece, slice(lb * CONV_B_LANES, (lb + 1) * CONV_B_LANES))
                for lb in range(d_conv // CONV_B_LANES)]

    a_items = _split_items(n_items, CONV_A_STEP_WEIGHTS)
    b_items = _split_items(n_items, CONV_B_STEP_WEIGHTS)

    def interleave(a_list, b_list, dot_piece):
        work = []
        for c in a_list:
            work += conv_a_pieces(c)
        for c in b_list:
            work += conv_b_pieces(c)
        n_sub = IN_BLOCK // SUB_BLOCK
        cuts = [-(-len(work) * (p + 1) // n_sub) for p in range(n_sub)]
        done = 0
        for p in range(n_sub):
            for piece in work[done:cuts[p]]:
                piece()
            done = cuts[p]
            dot_piece(slice(p * SUB_BLOCK, (p + 1) * SUB_BLOCK))

    def store_rows(full_ref, tail_ref, halo, stride, val, cols):
        for sg in range(nseg):
            full_ref[sg * stride + halo:(sg + 1) * stride, cols] = val[sg * seg:(sg + 1) * seg]
            tail_ref[sg, :, cols] = val[(sg + 1) * seg - halo:(sg + 1) * seg]

    @pl.when(j == 0)
    def _():
        _norm_rows(u_ref, x_ref, g_ref, rows)
        load_history(fulla_ref, ctxa_ref, HALO_A, stride_a)
        load_history(fullb_ref, ctxb_ref, HALO_B, stride_b)

        def glu(cols):
            a = z(w1_ref, b1_ref, cols) * _sigmoid(z(w2_ref, b2_ref, cols))
            store_rows(fulla_ref, taila_ref, HALO_A, stride_a, a, cols)

        interleave([], [], glu)

    @pl.when(j == 1)
    def _():
        def prod(cols):
            cx = z(w1_ref, b1_ref, cols) * z(w2_ref, b2_ref, cols)
            store_rows(fullb_ref, tailb_ref, HALO_B, stride_b, cx, cols)

        interleave(a_items[0], b_items[0], prod)

    @pl.when(j == 2)
    def _():
        def ident(cols):
            bb_ref[:, cols] = z(w1_ref, b1_ref, cols).astype(BF16)

        interleave(a_items[1], b_items[1], ident)

    gate_steps = range(FIRST_GATE_STEP, N_IN_STEPS)
    n_a = len(a_items[FIRST_GATE_STEP - 1])
    n_b = len(b_items[FIRST_GATE_STEP - 1])
    assert all(len(a_items[s - 1]) == n_a and len(b_items[s - 1]) == n_b for s in gate_steps)

    @pl.when(j >= FIRST_GATE_STEP)
    def _():
        def gate(cols):
            gates_ref[0, :, cols] = _sigmoid(z(w1_ref, b1_ref, cols)).astype(BF16)

        k = j - FIRST_GATE_STEP
        a0 = a_items[FIRST_GATE_STEP - 1].start
        b0 = b_items[FIRST_GATE_STEP - 1].start
        interleave([a0 + k * n_a + p for p in range(n_a)], [b0 + k * n_b + p for p in range(n_b)], gate)


def _inproj(x, g, w_in, b_in, ctx_a, ctx_b, caw, cab, lng, lnb, cbw, rows, nseg):
    b, s, d = x.shape
    d_conv = IN_BLOCK
    seg = rows // nseg
    assert nseg == 1 or s == rows, "stacked segments are whole sequences"
    crow = min(CONV_ROWS, seg)
    assert nseg == 1 or crow == seg, "stacked segments are one conv item each"
    assert seg >= HALO_A and seg % crow == 0
    return pl.pallas_call(
        functools.partial(_inproj_kernel, rows=rows, nseg=nseg),
        grid=(b, s // rows, N_IN_STEPS),
        in_specs=[
            pl.BlockSpec((1, rows, d), lambda bi, i, j: (bi, i, 0)),
            pl.BlockSpec((1, d), lambda bi, i, j: (0, 0)),
            pl.BlockSpec((d, IN_BLOCK), lambda bi, i, j: (0, _w1_block(j))),
            pl.BlockSpec((d, IN_BLOCK), lambda bi, i, j: (0, _w2_block(j))),
            pl.BlockSpec((1, IN_BLOCK), lambda bi, i, j: (0, _w1_block(j))),
            pl.BlockSpec((1, IN_BLOCK), lambda bi, i, j: (0, _w2_block(j))),
            pl.BlockSpec((nseg, HALO_A, d_conv), lambda bi, i, j: (bi, 0, 0)),
            pl.BlockSpec((nseg, HALO_B, d_conv), lambda bi, i, j: (bi, 0, 0)),
            _resident(caw.shape), _resident(cab.shape), _resident(lng.shape), _resident(lnb.shape),
            _resident(cbw.shape),
        ],
        out_specs=[
            pl.BlockSpec((1, rows, IN_BLOCK), lambda bi, i, j: (bi, i, jnp.maximum(j - 3, 0))),
            pl.BlockSpec((1, rows, 2 * d_conv), lambda bi, i, j: (bi, i, 0)),
            pl.BlockSpec((nseg, HALO_A, d_conv), lambda bi, i, j: (bi, 0, 0)),
            pl.BlockSpec((nseg, HALO_B, d_conv), lambda bi, i, j: (bi, 0, 0)),
        ],
        out_shape=[
            jax.ShapeDtypeStruct((b, s, N_GATE_BLOCKS * IN_BLOCK), BF16),
            jax.ShapeDtypeStruct((b, s, 2 * d_conv), BF16),
            jax.ShapeDtypeStruct((b * nseg, HALO_A, d_conv), F32),
            jax.ShapeDtypeStruct((b * nseg, HALO_B, d_conv), F32),
        ],
        scratch_shapes=[
            pltpu.VMEM((rows, d), BF16),
            pltpu.VMEM((nseg * (HALO_A + seg), d_conv), F32),
            pltpu.VMEM((nseg * (HALO_B + seg), d_conv), F32),
            pltpu.VMEM((rows, d_conv), BF16),
            pltpu.VMEM((rows, d_conv), F32),
            pltpu.VMEM((SUBLANES_F32 - 1, min(CONV_ROWS, seg) + HALO_A - SUBLANES_F32, d_conv), F32),
        ],
        compiler_params=_params(3),
        name="inproj",
    )(x, g, w_in, w_in, b_in, b_in, ctx_a, ctx_b, caw, cab, lng, lnb, cbw)


def _mixer_kernel(act_ref, sga_ref, sgb_ref, x_ref, wa_ref, ba_ref, wb_ref, wm_ref, gn_ref,
                  o_ref, hn_ref):
    d_conv = wa_ref.shape[0]
    a_out = _dot(act_ref[0, :, 0:d_conv], wa_ref[...]) + ba_ref[...]
    b_out = _dot(act_ref[0, :, d_conv:2 * d_conv], wb_ref[...])
    m = sga_ref[0].astype(F32) * a_out + sgb_ref[0].astype(F32) * b_out
    o = x_ref[0] + _dot(m.astype(BF16), wm_ref[...])
    o_ref[0] = o
    hn_ref[0] = _rms(o, gn_ref[...]).astype(BF16)


def _mixer(act, gates, x, wa, ba, wb, wm, g_next, rows):
    b, s, d = x.shape
    tile = pl.BlockSpec((1, rows, d), lambda bi, i: (bi, i, 0))
    return pl.pallas_call(
        _mixer_kernel,
        grid=(b, s // rows),
        in_specs=[
            pl.BlockSpec((1, rows, act.shape[2]), lambda bi, i: (bi, i, 0)),
            pl.BlockSpec((1, rows, d), lambda bi, i: (bi, i, 0)),
            pl.BlockSpec((1, rows, d), lambda bi, i: (bi, i, 1)),
            pl.BlockSpec((1, rows, d), lambda bi, i: (bi, i, 0)),
            _resident(wa.shape), _resident(ba.shape), _resident(wb.shape), _resident(wm.shape),
            _resident(g_next.shape),
        ],
        out_specs=[tile, tile],
        out_shape=[jax.ShapeDtypeStruct((b, s, d), F32), jax.ShapeDtypeStruct((b, s, d), BF16)],
        compiler_params=_params(2),
        name="mixer",
    )(act, gates, gates, x, wa, ba, wb, wm, g_next)


def _attn_kernel(x_ref, h_ref, wq_ref, k_ref, v_ref, wo_ref, gn_ref, o_ref, hn_ref, q_ref, att_ref,
                 *, nseg):
    rows, d = x_ref.shape[1], x_ref.shape[2]
    seg = rows // nseg
    hd = d // N_XHEADS
    q_ref[...] = _dot(h_ref[0], wq_ref[...]).astype(BF16)
    for sg in range(nseg):
        rs = slice(sg * seg, (sg + 1) * seg)
        for h in range(N_XHEADS):
            hs = slice(h * hd, (h + 1) * hd)
            s = lax.dot_general(q_ref[rs, hs], k_ref[sg, :, hs], (((1,), (1,)), ((), ())),
                                preferred_element_type=F32) * (hd ** -0.5)
            e = jnp.exp(s - jnp.max(s, axis=-1, keepdims=True))
            inv = 1.0 / jnp.sum(e, axis=-1, keepdims=True)
            att_ref[rs, hs] = (_dot(e.astype(BF16), v_ref[sg, :, hs]) * inv).astype(BF16)
    o = x_ref[0] + _dot(att_ref[...], wo_ref[...])
    o_ref[0] = o
    hn_ref[0] = _rms(o, gn_ref[...]).astype(BF16)


def _attn(x, h, wq, k, v, wo, g_next, rows, nseg):
    b, s, d = x.shape
    n_mem = k.shape[1]
    assert nseg == 1 or s == rows, "stacked segments are whole sequences"
    tile = pl.BlockSpec((1, rows, d), lambda bi, i: (bi, i, 0))
    return pl.pallas_call(
        functools.partial(_attn_kernel, nseg=nseg),
        grid=(b, s // rows),
        in_specs=[
            tile, tile, _resident(wq.shape),
            pl.BlockSpec((nseg, n_mem, d), lambda bi, i: (bi, 0, 0)),
            pl.BlockSpec((nseg, n_mem, d), lambda bi, i: (bi, 0, 0)),
            _resident(wo.shape), _resident(g_next.shape),
        ],
        out_specs=[tile, tile],
        out_shape=[jax.ShapeDtypeStruct((b, s, d), F32), jax.ShapeDtypeStruct((b, s, d), BF16)],
        scratch_shapes=[pltpu.VMEM((rows, d), BF16)] * 2,
        compiler_params=_params(2),
        name="attn",
    )(x, h, wq, k, v, wo, g_next)


def _ffn_kernel(x_ref, h_ref, wg_ref, wu_ref, wd_ref, gf_ref, o_ref, acc_ref):
    j = pl.program_id(2)
    last = pl.num_programs(2) - 1

    @pl.when(j == 0)
    def _():
        acc_ref[...] = jnp.zeros_like(acc_ref)

    def chunk():
        h = h_ref[0]
        gate = _dot(h, wg_ref[...])
        act = (gate * _sigmoid(gate) * _dot(h, wu_ref[...])).astype(BF16)
        return _dot(act, wd_ref[...])

    @pl.when(j < last)
    def _():
        acc_ref[...] += chunk()

    @pl.when(j == last)
    def _():
        o_ref[0] = _rms(x_ref[0] + acc_ref[...] + chunk(), gf_ref[...])


def _ffn(x, h, wg, wu, wd, gf, rows):
    b, s, d = x.shape
    d_ff = wg.shape[1]
    tile = pl.BlockSpec((1, rows, d), lambda bi, i, j: (bi, i, 0))
    return pl.pallas_call(
        _ffn_kernel,
        grid=(b, s // rows, d_ff // FF_BLOCK),
        in_specs=[
            tile, tile,
            pl.BlockSpec((d, FF_BLOCK), lambda bi, i, j: (0, j)),
            pl.BlockSpec((d, FF_BLOCK), lambda bi, i, j: (0, j)),
            pl.BlockSpec((FF_BLOCK, d), lambda bi, i, j: (j, 0)),
            pl.BlockSpec((1, d), lambda bi, i, j: (0, 0)),
        ],
        out_specs=tile,
        out_shape=jax.ShapeDtypeStruct((b, s, d), F32),
        scratch_shapes=[pltpu.VMEM((rows, d), F32)],
        compiler_params=_params(3),
        name="ffn",
    )(x, h, wg, wu, wd, gf)


def _pad_rows_front(a, rows):
    return jnp.pad(a, ((0, 0), (rows - a.shape[1], 0), (0, 0)))


def _layer(x, ctx_a, ctx_b, k, v, w, stack, tiles):
    b, s, d = x.shape
    t_in, t_mix, t_attn, t_ffn = tiles
    ctx_a = _pad_rows_front(ctx_a, HALO_A)
    ctx_b = _pad_rows_front(ctx_b, HALO_B)
    xs = x.reshape(1, b * s, d) if stack else x
    nseg = b if stack else 1
    gates, act, tail_a, tail_b = _inproj(
        xs, w["norm_mix_g"], w["w_in"], w["b_in"], ctx_a, ctx_b,
        w["conv_a_w"], w["conv_a_b"], w["ln_a_g"], w["ln_a_b"], w["conv_b_w"], t_in, nseg)
    xs, hs = _mixer(act, gates, xs, w["w_a_out"], w["b_a_out"], w["w_b_out"], w["w_mix_out"],
                    w["norm_x_g"], t_mix)
    xa, ha = _attn(xs, hs, w["w_q"], k, v, w["w_o"], w["norm_ffn_g"], t_attn, nseg)
    y = _ffn(xa, ha, w["w_gate"], w["w_up"], w["w_down"], w["norm_final_g"], t_ffn)
    new_a = tail_a[:, HALO_A - (CONV_A_WIDTH - 1):, :]
    new_b = tail_b[:, HALO_B - (CONV_B_WIDTH - 1):, :]
    return y.reshape(b, s, d), new_a, new_b


def kernel(x_prompt, x_sample, state_conv_a, state_conv_b, cache_mem_k, cache_mem_v, mem_prompt,
           norm_mix_g, w_in, b_in, conv_a_w, conv_a_b, ln_a_g, ln_a_b, w_a_out, b_a_out,
           conv_b_w, w_b_out, w_mix_out, norm_x_g, norm_mem_g, w_q, w_k, w_v, w_o,
           norm_ffn_g, w_gate, w_up, w_down, norm_final_g):
    assert norm_mix_g.shape[0] == 1, "single-layer trunk"
    row = lambda a: a.reshape(1, -1)
    w = {
        "norm_mix_g": row(norm_mix_g[0]), "w_in": w_in[0].astype(BF16), "b_in": row(b_in[0]),
        "conv_a_w": conv_a_w[0], "conv_a_b": row(conv_a_b[0]),
        "ln_a_g": row(ln_a_g[0]), "ln_a_b": row(ln_a_b[0]),
        "w_a_out": w_a_out[0].astype(BF16), "b_a_out": row(b_a_out[0]),
        "conv_b_w": conv_b_w[0], "w_b_out": w_b_out[0].astype(BF16),
        "w_mix_out": w_mix_out[0].astype(BF16),
        "norm_x_g": row(norm_x_g[0]), "w_q": w_q[0].astype(BF16), "w_o": w_o[0].astype(BF16),
        "norm_ffn_g": row(norm_ffn_g[0]), "w_gate": w_gate[0].astype(BF16),
        "w_up": w_up[0].astype(BF16), "w_down": w_down[0].astype(BF16),
        "norm_final_g": row(norm_final_g),
    }
    bp, sp, d = x_prompt.shape
    bs, ss, _ = x_sample.shape
    d_conv = conv_a_w.shape[-1]
    hd = d // N_XHEADS

    kp, vp = _memkv(mem_prompt, row(norm_mem_g[0]), w_k[0].astype(BF16), w_v[0].astype(BF16))
    zero_a = jnp.zeros((bp, CONV_A_WIDTH - 1, d_conv), F32)
    zero_b = jnp.zeros((bp, CONV_B_WIDTH - 1, d_conv), F32)
    y_p, na_p, nb_p = _layer(x_prompt, zero_a, zero_b, kp.astype(BF16), vp.astype(BF16), w,
                             False, (512, 256, 512, 512))

    n_mem = cache_mem_k.shape[2]
    ks = cache_mem_k[0].reshape(bs, n_mem, d).astype(BF16)
    vs = cache_mem_v[0].reshape(bs, n_mem, d).astype(BF16)
    y_s, na_s, nb_s = _layer(x_sample, state_conv_a[0], state_conv_b[0], ks, vs, w,
                             True, (bs * ss,) * 4)

    kv_shape = (1, bp, n_mem, N_XHEADS, hd)
    return (y_p, y_s, na_p[None], nb_p[None], kp.reshape(kv_shape), vp.reshape(kv_shape),
            na_s[None], nb_s[None])
```
